```python
import jax, jax.numpy as jnp
from jax import lax
import numpy as np

D_MODEL = 1024
BATCH = 8
SEQ = 4096
DEPTH = 1

CHUNK = 64
LEFT_CHUNKS = 8
BAND = (LEFT_CHUNKS + 1) * CHUNK
N_HEADS = 8
HEAD_DIM = 64
D_ATTN = N_HEADS * HEAD_DIM
D_CONV = 512
CONV_K = 31
MAX_REL = 128
D_FF = 2816
FFN_CONV_K = 3
EPS = 1e-6
NEG_INF = -1e30
IN_WIDTHS = (D_ATTN, D_ATTN, D_ATTN, D_CONV, D_CONV, D_MODEL, D_MODEL)
D_IN = D_ATTN * 3 + D_CONV * 2 + D_MODEL * 2

kernel_name = "hybrid_chunked_attn_conformer_conv_convffn_block"


def rms_norm(x, g):
    xf = x.astype(jnp.float32)
    y = xf * lax.rsqrt(jnp.mean(xf * xf, axis=-1, keepdims=True) + EPS)
    return (y * g.astype(jnp.float32)).astype(x.dtype)


def layer_norm(x, g, b):
    xf = x.astype(jnp.float32)
    mu = jnp.mean(xf, axis=-1, keepdims=True)
    var = jnp.mean(jnp.square(xf - mu), axis=-1, keepdims=True)
    y = (xf - mu) * lax.rsqrt(var + EPS)
    return (y * g.astype(jnp.float32) + b.astype(jnp.float32)).astype(x.dtype)


def causal_dwconv(x, w, b):
    k = w.shape[0]
    y = lax.conv_general_dilated(
        x, w[:, None, :].astype(x.dtype), window_strides=(1,), padding=[(k - 1, 0)],
        dimension_numbers=('NWC', 'WIO', 'NWC'), feature_group_count=x.shape[-1])
    return y + b


def chunk_band(t):
    b, s, h, dh = t.shape
    nc = s // CHUNK
    tc = t.reshape(b, nc, CHUNK, h, dh)
    tp = jnp.pad(tc, ((0, 0), (LEFT_CHUNKS, 0), (0, 0), (0, 0), (0, 0)))
    band = jnp.stack([tp[:, j:j + nc] for j in range(LEFT_CHUNKS + 1)], axis=2)
    return band.reshape(b, nc, BAND, h, dh)


def chunked_rel_attention(q, k, v, rel_bias):
    b, s, _ = q.shape
    nc = s // CHUNK
    qc = q.reshape(b, nc, CHUNK, N_HEADS, HEAD_DIM)
    kb = chunk_band(k.reshape(b, s, N_HEADS, HEAD_DIM))
    vb = chunk_band(v.reshape(b, s, N_HEADS, HEAD_DIM))
    scores = jnp.einsum('bcqhd,bckhd->bhcqk', qc, kb).astype(jnp.float32) * (HEAD_DIM ** -0.5)
    qi = jnp.arange(CHUNK)
    kj = jnp.arange(BAND)
    rel = LEFT_CHUNKS * CHUNK + qi[:, None] - kj[None, :]
    idx = jnp.clip(rel, -MAX_REL, MAX_REL) + MAX_REL
    bias = rel_bias[:, idx].astype(jnp.float32)
    scores = scores + bias[None, :, None, :, :]
    key_chunk = jnp.arange(nc)[:, None] - LEFT_CHUNKS + (kj // CHUNK)[None, :]
    valid = key_chunk >= 0
    scores = jnp.where(valid[None, None, :, None, :], scores, NEG_INF)
    probs = jax.nn.softmax(scores, axis=-1).astype(v.dtype)
    out = jnp.einsum('bhcqk,bckhd->bcqhd', probs, vb)
    return out.reshape(b, s, D_ATTN)


def token_mixer(h, w_in, b_in, rel_bias, w_attn_o, w_dw, b_dw, g_ln, b_ln,
                w_conv_o, b_conv_o, w_mix_o):
    z = h @ w_in + b_in
    splits = list(np.cumsum(IN_WIDTHS)[:-1])
    q, k, v, glu_a, glu_b, gate_a, gate_b = jnp.split(z, splits, axis=-1)
    a = chunked_rel_attention(q, k, v, rel_bias) @ w_attn_o
    u = glu_a * jax.nn.sigmoid(glu_b)
    u = causal_dwconv(u, w_dw, b_dw)
    u = jax.nn.silu(layer_norm(u, g_ln, b_ln))
    cb = u @ w_conv_o + b_conv_o
    y = jax.nn.sigmoid(gate_a) * a + jax.nn.sigmoid(gate_b) * cb
    return y @ w_mix_o


def conv_ffn(h, w_up, w_dw, b_dw, w_down):
    u = causal_dwconv(h @ w_up, w_dw, b_dw)
    val, gt = jnp.split(u, 2, axis=-1)
    return (jax.nn.gelu(gt) * val) @ w_down


def setup_inputs(seed: int = 0) -> dict:
    key = jax.random.key(seed)
    ks = jax.random.split(key, 26)
    L = DEPTH

    def nrm(k, shape, scale):
        return jax.random.normal(k, shape, jnp.float32) * scale

    def gain(k, shape):
        return 1.0 + 0.1 * jax.random.normal(k, shape, jnp.float32)

    return {
        "x": nrm(ks[0], (BATCH, SEQ, D_MODEL), 1.0),
        "c": nrm(ks[1], (BATCH, D_MODEL), 1.0),
        "w_ada": nrm(ks[2], (L, D_MODEL, 6 * D_MODEL), 0.5 * D_MODEL ** -0.5),
        "b_ada": nrm(ks[3], (L, 6 * D_MODEL), 0.01),
        "g_pre_mix": gain(ks[4], (L, D_MODEL)),
        "g_post_mix": gain(ks[5], (L, D_MODEL)),
        "w_in": nrm(ks[6], (L, D_MODEL, D_IN), D_MODEL ** -0.5),
        "b_in": nrm(ks[7], (L, D_IN), 0.01),
        "rel_bias": nrm(ks[8], (L, N_HEADS, 2 * MAX_REL + 1), 0.5),
        "w_attn_o": nrm(ks[9], (L, D_ATTN, D_MODEL), D_ATTN ** -0.5),
        "w_dw_conv": nrm(ks[10], (L, CONV_K, D_CONV), CONV_K ** -0.5),
        "b_dw_conv": nrm(ks[11], (L, D_CONV), 0.01),
        "g_conv_ln": gain(ks[12], (L, D_CONV)),
        "b_conv_ln": nrm(ks[13], (L, D_CONV), 0.01),
        "w_conv_o": nrm(ks[14], (L, D_CONV, D_MODEL), D_CONV ** -0.5),
        "b_conv_o": nrm(ks[15], (L, D_MODEL), 0.01),
        "w_mix_o": nrm(ks[16], (L, D_MODEL, D_MODEL), D_MODEL ** -0.5),
        "g_pre_ffn": gain(ks[17], (L, D_MODEL)),
        "g_post_ffn": gain(ks[18], (L, D_MODEL)),
        "w_up": nrm(ks[19], (L, D_MODEL, 2 * D_FF), D_MODEL ** -0.5),
        "w_dw_ffn": nrm(ks[20], (L, FFN_CONV_K, 2 * D_FF), FFN_CONV_K ** -0.5),
        "b_dw_ffn": nrm(ks[21], (L, 2 * D_FF), 0.01),
        "w_down": nrm(ks[22], (L, D_FF, D_MODEL), D_FF ** -0.5),
    }


def reference(x, c, w_ada, b_ada, g_pre_mix, g_post_mix, w_in, b_in, rel_bias,
              w_attn_o, w_dw_conv, b_dw_conv, g_conv_ln, b_conv_ln, w_conv_o,
              b_conv_o, w_mix_o, g_pre_ffn, g_post_ffn, w_up, w_dw_ffn, b_dw_ffn,
              w_down):
    c_act = jax.nn.silu(c)
    for l in range(DEPTH):
        mod = c_act @ w_ada[l] + b_ada[l]
        sh_m, sc_m, gt_m, sh_f, sc_f, gt_f = [m[:, None, :] for m in jnp.split(mod, 6, axis=-1)]
        h = rms_norm(x, g_pre_mix[l]) * (1.0 + sc_m) + sh_m
        y = token_mixer(h, w_in[l], b_in[l], rel_bias[l], w_attn_o[l], w_dw_conv[l],
                        b_dw_conv[l], g_conv_ln[l], b_conv_ln[l], w_conv_o[l],
                        b_conv_o[l], w_mix_o[l])
        x = x + gt_m * rms_norm(y, g_post_mix[l])
        h = rms_norm(x, g_pre_ffn[l]) * (1.0 + sc_f) + sh_f
        y = conv_ffn(h, w_up[l], w_dw_ffn[l], b_dw_ffn[l], w_down[l])
        x = x + gt_f * rms_norm(y, g_post_ffn[l])
    return x
```

```python
import functools

import jax
import jax.numpy as jnp
import numpy as np
from jax import lax
from jax.experimental import pallas as pl
from jax.experimental.pallas import tpu as pltpu

D_MODEL = 1024
CHUNK = 64
LEFT_CHUNKS = 8
CTX = LEFT_CHUNKS * CHUNK
N_HEADS = 8
HEAD_DIM = 64
D_ATTN = N_HEADS * HEAD_DIM
D_CONV = 512
CONV_K = 31
MAX_REL = 128
D_FF = 2816
FFN_CONV_K = 3
EPS = 1e-6
NEG_INF = -1e30

PAIR = 2 * CHUNK
PAIR_KEYS = PAIR + CTX
SLAB = 256
HEADS_PER_SLAB = SLAB // HEAD_DIM
N_SLABS = D_ATTN // SLAB
CONV_HIST = 32
FFN_HIST = 8
FFN_COLS = 256

TILE_MIX = 512
TILE_FFN = 512
VMEM_LIMIT_BYTES = 56 * 1024 * 1024

_F32 = jnp.float32
_BF16 = jnp.bfloat16


def _dot(a, b):
    return jnp.dot(a, b, preferred_element_type=_F32)


def _dot_nt(a, b):
    return lax.dot_general(a, b, (((1,), (1,)), ((), ())), preferred_element_type=_F32)


def _rms(x):
    return x * lax.rsqrt(jnp.mean(x * x, axis=-1, keepdims=True) + EPS)


def _adaln_kernel(c_ref, w_ref, b_ref, o_ref):
    c = c_ref[...]
    o_ref[...] = _dot(c * jax.nn.sigmoid(c), w_ref[...]) + b_ref[...]


def _adaln(c, w_ada, b_ada):
    b, d = c.shape
    n = w_ada.shape[1]
    tn = 1024
    return pl.pallas_call(
        _adaln_kernel,
        out_shape=jax.ShapeDtypeStruct((b, n), _F32),
        grid=(n // tn,),
        in_specs=[
            pl.BlockSpec((b, d), lambda j: (0, 0)),
            pl.BlockSpec((d, tn), lambda j: (0, j)),
            pl.BlockSpec((1, tn), lambda j: (0, j)),
        ],
        out_specs=pl.BlockSpec((b, tn), lambda j: (0, j)),
        name="adaln",
    )(c, w_ada, b_ada.reshape(1, n))


def _mixer_kernel(x_ref, mod_ref, gpre_ref, gpost_ref, w_in_ref, b_in_ref, bias_ref,
                  w_ao_ref, w_dw_ref, b_dw_ref, g_ln_ref, b_ln_ref, w_co_ref, b_co_ref,
                  w_mo_ref, o_ref,
                  h_buf, q_buf, k_buf, v_buf, a_buf, u_buf, c_buf):
    t = TILE_MIX
    s = pl.program_id(1)

    @pl.when(s == 0)
    def _():
        k_buf[0:CTX, :] = jnp.zeros((CTX, D_ATTN), _BF16)
        v_buf[0:CTX, :] = jnp.zeros((CTX, D_ATTN), _BF16)
        u_buf[0:CONV_HIST, :] = jnp.zeros((CONV_HIST, D_CONV), _F32)

    shift = mod_ref[0:1, :]
    scale = mod_ref[1:2, :]
    gate = mod_ref[2:3, :]

    x = x_ref[...]
    h = _rms(x) * gpre_ref[...] * (1.0 + scale) + shift
    h_buf[...] = h.astype(_BF16)

    def proj(c0, width):
        return _dot(h_buf[...], w_in_ref[:, c0:c0 + width]) + b_in_ref[:, c0:c0 + width]

    lane = lax.broadcasted_iota(jnp.int32, (PAIR, SLAB), 1)
    for sl in range(N_SLABS):
        q = proj(sl * SLAB, SLAB) * (HEAD_DIM ** -0.5)
        for p in range(t // PAIR):
            qp = q[p * PAIR:(p + 1) * PAIR, :]
            for hh in range(HEADS_PER_SLAB):
                head_lanes = (lane >= hh * HEAD_DIM) & (lane < (hh + 1) * HEAD_DIM)
                q_buf[p, sl, hh * PAIR:(hh + 1) * PAIR, :] = jnp.where(head_lanes, qp, 0.0).astype(_BF16)
    k_buf[CTX:CTX + t, :] = proj(D_ATTN, D_ATTN).astype(_BF16)
    v_buf[CTX:CTX + t, :] = proj(2 * D_ATTN, D_ATTN).astype(_BF16)
    glu_a = proj(3 * D_ATTN, D_CONV)
    glu_b = proj(3 * D_ATTN + D_CONV, D_CONV)
    u_buf[CONV_HIST:CONV_HIST + t, :] = glu_a * jax.nn.sigmoid(glu_b)

    key_col = lax.broadcasted_iota(jnp.int32, (1, PAIR_KEYS), 1)

    def pair_body(p, carry):
        r0 = pl.multiple_of(p * PAIR, PAIR)
        first_key = s * t - CTX + r0
        seq_mask = jnp.where(first_key + key_col >= 0, 0.0, NEG_INF)
        for sl in range(N_SLABS):
            c0 = sl * SLAB
            qs = q_buf[p, sl]
            ks = k_buf[pl.ds(r0, PAIR_KEYS), c0:c0 + SLAB]
            vs = v_buf[pl.ds(r0, PAIR_KEYS), c0:c0 + SLAB]
            sc = _dot_nt(qs, ks) + bias_ref[sl] + seq_mask
            m = jnp.max(sc, axis=-1, keepdims=True)
            e = jnp.exp(sc - m)
            l = jnp.sum(e, axis=-1, keepdims=True)
            o = _dot(e.astype(_BF16), vs) * (1.0 / l)
            out = jnp.zeros((PAIR, SLAB), _F32)
            for hh in range(HEADS_PER_SLAB):
                head_lanes = (lane >= hh * HEAD_DIM) & (lane < (hh + 1) * HEAD_DIM)
                out = out + jnp.where(head_lanes, o[hh * PAIR:(hh + 1) * PAIR, :], 0.0)
            a_buf[pl.ds(r0, PAIR), c0:c0 + SLAB] = out.astype(_BF16)
        return carry

    lax.fori_loop(0, t // PAIR, pair_body, 0)
    a = _dot(a_buf[...], w_ao_ref[...])

    rows = 64
    for rb in range(t // rows):
        acc = jnp.zeros((rows, D_CONV), _F32) + b_dw_ref[...]
        for k in range(CONV_K):
            off = rb * rows + CONV_HIST - (CONV_K - 1) + k
            acc = acc + w_dw_ref[k:k + 1, :] * u_buf[off:off + rows, :]
        mu = jnp.mean(acc, axis=-1, keepdims=True)
        cen = acc - mu
        var = jnp.mean(cen * cen, axis=-1, keepdims=True)
        y = cen * lax.rsqrt(var + EPS) * g_ln_ref[...] + b_ln_ref[...]
        y = y * jax.nn.sigmoid(y)
        c_buf[rb * rows:(rb + 1) * rows, :] = y.astype(_BF16)
    cb = _dot(c_buf[...], w_co_ref[...]) + b_co_ref[...]

    gate_a = jax.nn.sigmoid(proj(3 * D_ATTN + 2 * D_CONV, D_MODEL))
    gate_b = jax.nn.sigmoid(proj(3 * D_ATTN + 2 * D_CONV + D_MODEL, D_MODEL))
    y = _dot((gate_a * a + gate_b * cb).astype(_BF16), w_mo_ref[...])
    o_ref[...] = x + gate * (_rms(y) * gpost_ref[...])

    k_buf[0:CTX, :] = k_buf[t:t + CTX, :]
    v_buf[0:CTX, :] = v_buf[t:t + CTX, :]
    u_buf[0:CONV_HIST, :] = u_buf[t:t + CONV_HIST, :]


def _rel_bias_table(rel_bias):
    r = np.arange(PAIR)[:, None]
    j = np.arange(PAIR_KEYS)[None, :]
    first = (r // CHUNK) * CHUNK
    kj = j - first
    in_band = (kj >= 0) & (kj < CTX + CHUNK)
    rel = CTX + (r % CHUNK) - kj
    idx = np.clip(rel, -MAX_REL, MAX_REL) + MAX_REL
    table = rel_bias[:, idx]
    table = jnp.where(in_band[None], table, NEG_INF)
    return table.reshape(N_SLABS, HEADS_PER_SLAB * PAIR, PAIR_KEYS)


def _const_spec(shape):
    zeros = (0,) * len(shape)
    return pl.BlockSpec(shape, lambda b, s: zeros, pipeline_mode=pl.Buffered(1))


def _mixer(x, mod, g_pre, g_post, w_in, b_in, bias_tab, w_ao, w_dw, b_dw, g_ln, b_ln, w_co, b_co, w_mo):
    bsz, seq, d = x.shape
    t = TILE_MIX
    consts = [g_pre, g_post, w_in, b_in, bias_tab, w_ao, w_dw, b_dw, g_ln, b_ln, w_co, b_co, w_mo]
    return pl.pallas_call(
        _mixer_kernel,
        out_shape=jax.ShapeDtypeStruct(x.shape, _F32),
        grid=(bsz, seq // t),
        in_specs=[
            pl.BlockSpec((None, t, d), lambda b, s: (b, s, 0)),
            pl.BlockSpec((None, 3, d), lambda b, s: (b, 0, 0)),
        ] + [_const_spec(a.shape) for a in consts],
        out_specs=pl.BlockSpec((None, t, d), lambda b, s: (b, s, 0)),
        scratch_shapes=[
            pltpu.VMEM((t, d), _BF16),
            pltpu.VMEM((t // PAIR, N_SLABS, HEADS_PER_SLAB * PAIR, SLAB), _BF16),
            pltpu.VMEM((CTX + t, D_ATTN), _BF16),
            pltpu.VMEM((CTX + t, D_ATTN), _BF16),
            pltpu.VMEM((t, D_ATTN), _BF16),
            pltpu.VMEM((CONV_HIST + t, D_CONV), _F32),
            pltpu.VMEM((t, D_CONV), _BF16),
        ],
        compiler_params=pltpu.CompilerParams(
            dimension_semantics=("arbitrary", "arbitrary"),
            vmem_limit_bytes=VMEM_LIMIT_BYTES),
        name="mixer",
    )(x, mod, *consts)


def _ffn_kernel(x_ref, mod_ref, gpre_ref, gpost_ref, w_up_ref, w_dw_ref, b_dw_ref, w_down_ref,
                o_ref, h_buf, act_buf, hist_buf, win_buf):
    t = TILE_FFN
    s = pl.program_id(1)

    @pl.when(s == 0)
    def _():
        hist_buf[...] = jnp.zeros(hist_buf.shape, _F32)

    shift = mod_ref[0:1, :]
    scale = mod_ref[1:2, :]
    gate = mod_ref[2:3, :]

    x = x_ref[...]
    h_buf[...] = (_rms(x) * gpre_ref[...] * (1.0 + scale) + shift).astype(_BF16)

    def conv_cols(c0):
        cols = slice(c0, c0 + FFN_COLS)
        up = _dot(h_buf[...], w_up_ref[:, cols])
        win_buf[0:FFN_HIST, :] = hist_buf[:, cols]
        win_buf[FFN_HIST:FFN_HIST + t, :] = up
        hist_buf[:, cols] = up[t - FFN_HIST:t, :]
        y = b_dw_ref[:, cols] + w_dw_ref[FFN_CONV_K - 1:FFN_CONV_K, cols] * up
        for k in range(FFN_CONV_K - 1):
            off = FFN_HIST - (FFN_CONV_K - 1) + k
            y = y + w_dw_ref[k:k + 1, cols] * win_buf[off:off + t, :]
        return y

    for c in range(D_FF // FFN_COLS):
        val = conv_cols(c * FFN_COLS)
        gt = conv_cols(D_FF + c * FFN_COLS)
        act_buf[:, c * FFN_COLS:(c + 1) * FFN_COLS] = (jax.nn.gelu(gt) * val).astype(_BF16)

    y = _dot(act_buf[...], w_down_ref[...])
    o_ref[...] = x + gate * (_rms(y) * gpost_ref[...])


def _ffn(x, mod, g_pre, g_post, w_up, w_dw, b_dw, w_down):
    bsz, seq, d = x.shape
    t = TILE_FFN
    consts = [g_pre, g_post, w_up, w_dw, b_dw, w_down]
    return pl.pallas_call(
        _ffn_kernel,
        out_shape=jax.ShapeDtypeStruct(x.shape, _F32),
        grid=(bsz, seq // t),
        in_specs=[
            pl.BlockSpec((None, t, d), lambda b, s: (b, s, 0)),
            pl.BlockSpec((None, 3, d), lambda b, s: (b, 0, 0)),
        ] + [_const_spec(a.shape) for a in consts],
        out_specs=pl.BlockSpec((None, t, d), lambda b, s: (b, s, 0)),
        scratch_shapes=[
            pltpu.VMEM((t, d), _BF16),
            pltpu.VMEM((t, D_FF), _BF16),
            pltpu.VMEM((FFN_HIST, 2 * D_FF), _F32),
            pltpu.VMEM((FFN_HIST + t, FFN_COLS), _F32),
        ],
        compiler_params=pltpu.CompilerParams(
            dimension_semantics=("arbitrary", "arbitrary"),
            vmem_limit_bytes=VMEM_LIMIT_BYTES),
        name="ffn",
    )(x, mod, *consts)


def kernel(x, c, w_ada, b_ada, g_pre_mix, g_post_mix, w_in, b_in, rel_bias, w_attn_o, w_dw_conv,
           b_dw_conv, g_conv_ln, b_conv_ln, w_conv_o, b_conv_o, w_mix_o, g_pre_ffn, g_post_ffn,
           w_up, w_dw_ffn, b_dw_ffn, w_down):
    depth = w_ada.shape[0]
    bsz = x.shape[0]
    assert x.shape[1] % TILE_MIX == 0 and x.shape[1] % TILE_FFN == 0
    row = lambda v: v.reshape(1, -1)
    for l in range(depth):
        mod = _adaln(c, w_ada[l], b_ada[l]).reshape(bsz, 6, D_MODEL)
        x = _mixer(x, mod[:, 0:3], row(g_pre_mix[l]), row(g_post_mix[l]),
                   w_in[l].astype(_BF16), row(b_in[l]), _rel_bias_table(rel_bias[l]),
                   w_attn_o[l].astype(_BF16), w_dw_conv[l], row(b_dw_conv[l]),
                   row(g_conv_ln[l]), row(b_conv_ln[l]), w_conv_o[l].astype(_BF16),
                   row(b_conv_o[l]), w_mix_o[l].astype(_BF16))
        x = _ffn(x, mod[:, 3:6], row(g_pre_ffn[l]), row(g_post_ffn[l]),
                 w_up[l].astype(_BF16), w_dw_ffn[l], row(b_dw_ffn[l]), w_down[l].astype(_BF16))
    return x
```

```python
import jax
import jax.numpy as jnp
import numpy as np
from jax import lax
from jax.experimental import pallas as pl
from jax.experimental.pallas import tpu as pltpu

D_MODEL = 1024
CHUNK = 64
LEFT_CHUNKS = 8
CTX = LEFT_CHUNKS * CHUNK
BAND = CTX + CHUNK
N_HEADS = 8
HEAD_DIM = 64
D_ATTN = N_HEADS * HEAD_DIM
D_CONV = 512
CONV_K = 31
MAX_REL = 128
D_FF = 2816
FFN_CONV_K = 3
EPS = 1e-6
NEG_INF = -1e30

LANES = 128
PAIR = 2 * CHUNK
PAIR_KEYS = PAIR + CTX
SLAB = 256
HEADS_PER_SLAB = SLAB // HEAD_DIM
N_SLABS = D_ATTN // SLAB
CONV_HIST = 32
CONV_ROWS = 64
FFN_HIST = 8
FFN_COLS = 256

TILE_MIX = 512
TILE_FFN = 512
VMEM_LIMIT_BYTES = 56 * 1024 * 1024

_F32 = jnp.float32
_BF16 = jnp.bfloat16


def _dot(a, b):
    return jnp.dot(a, b, preferred_element_type=_F32)


def _dot_nt(a, b):
    return lax.dot_general(a, b, (((1,), (1,)), ((), ())), preferred_element_type=_F32)


def _rms(x):
    return x * lax.rsqrt(jnp.mean(x * x, axis=-1, keepdims=True) + EPS)


def _adaln_kernel(c_ref, w_ref, b_ref, o_ref):
    c = c_ref[...]
    o_ref[...] = _dot(c * jax.nn.sigmoid(c), w_ref[...]) + b_ref[...]


def _adaln(c, w_ada, b_ada):
    b, d = c.shape
    n = w_ada.shape[1]
    tn = 1024
    return pl.pallas_call(
        _adaln_kernel,
        out_shape=jax.ShapeDtypeStruct((b, n), _F32),
        grid=(n // tn,),
        in_specs=[
            pl.BlockSpec((b, d), lambda j: (0, 0)),
            pl.BlockSpec((d, tn), lambda j: (0, j)),
            pl.BlockSpec((1, tn), lambda j: (0, j)),
        ],
        out_specs=pl.BlockSpec((b, tn), lambda j: (0, j)),
        name="adaln",
    )(c, w_ada, b_ada.reshape(1, n))


def _mixer_kernel(x_ref, mod_ref, gpre_ref, gpost_ref, w_in_ref, b_in_ref, bias_ref,
                  w_ao_ref, w_dw_ref, b_dw_ref, g_ln_ref, b_ln_ref, w_co_ref, b_co_ref,
                  w_mo_ref, o_ref,
                  h_buf, q_buf, k_buf, v_buf, a_buf, u_buf, c_buf):
    t = TILE_MIX
    s = pl.program_id(1)
    n_conv_slabs = D_CONV // LANES

    @pl.when(s == 0)
    def _():
        k_buf[0:CTX, :] = jnp.zeros((CTX, D_ATTN), _BF16)
        v_buf[0:CTX, :] = jnp.zeros((CTX, D_ATTN), _BF16)
        u_buf[:, 0:CONV_HIST, :] = jnp.zeros((n_conv_slabs, CONV_HIST, LANES), _F32)

    @pl.when(s > 0)
    def _():
        k_buf[0:CTX, :] = k_buf[t:t + CTX, :]
        v_buf[0:CTX, :] = v_buf[t:t + CTX, :]
        u_buf[:, 0:CONV_HIST, :] = u_buf[:, t:t + CONV_HIST, :]

    shift = mod_ref[0:1, :]
    scale = mod_ref[1:2, :]
    gate = mod_ref[2:3, :]

    x = x_ref[...]
    h_buf[...] = (_rms(x) * gpre_ref[...] * (1.0 + scale) + shift).astype(_BF16)

    def proj(c0, width):
        return _dot(h_buf[...], w_in_ref[:, c0:c0 + width]) + b_in_ref[:, c0:c0 + width]

    lane = lax.broadcasted_iota(jnp.int32, (PAIR, SLAB), 1)
    head_lanes = [(lane >= hh * HEAD_DIM) & (lane < (hh + 1) * HEAD_DIM) for hh in range(HEADS_PER_SLAB)]
    for sl in range(N_SLABS):
        q = proj(sl * SLAB, SLAB) * (HEAD_DIM ** -0.5)
        for p in range(t // PAIR):
            qp = q[p * PAIR:(p + 1) * PAIR, :]
            for hh in range(HEADS_PER_SLAB):
                q_buf[p, sl, hh * PAIR:(hh + 1) * PAIR, :] = jnp.where(head_lanes[hh], qp, 0.0).astype(_BF16)
    k_buf[CTX:CTX + t, :] = proj(D_ATTN, D_ATTN).astype(_BF16)
    v_buf[CTX:CTX + t, :] = proj(2 * D_ATTN, D_ATTN).astype(_BF16)
    u = proj(3 * D_ATTN, D_CONV) * jax.nn.sigmoid(proj(3 * D_ATTN + D_CONV, D_CONV))
    for cs in range(n_conv_slabs):
        u_buf[cs, CONV_HIST:CONV_HIST + t, :] = u[:, cs * LANES:(cs + 1) * LANES]

    key_col = lax.broadcasted_iota(jnp.int32, (1, PAIR_KEYS), 1)

    def attention_pair(p):
        r0 = p * PAIR
        seq_mask = jnp.where(s * t - CTX + r0 + key_col >= 0, 0.0, NEG_INF)
        for sl in range(N_SLABS):
            c0 = sl * SLAB
            ks = k_buf[r0:r0 + PAIR_KEYS, c0:c0 + SLAB]
            vs = v_buf[r0:r0 + PAIR_KEYS, c0:c0 + SLAB]
            sc = _dot_nt(q_buf[p, sl], ks) + bias_ref[sl] + seq_mask
            e = jnp.exp(sc - jnp.max(sc, axis=-1, keepdims=True))
            l = jnp.sum(e, axis=-1, keepdims=True)
            o = _dot(e.astype(_BF16), vs) * (1.0 / l)
            out = jnp.where(head_lanes[0], o[0:PAIR, :], 0.0)
            for hh in range(1, HEADS_PER_SLAB):
                out = out + jnp.where(head_lanes[hh], o[hh * PAIR:(hh + 1) * PAIR, :], 0.0)
            a_buf[r0:r0 + PAIR, c0:c0 + SLAB] = out.astype(_BF16)

    def conv_rows(r0):
        accs = []
        for cs in range(n_conv_slabs):
            lanes = slice(cs * LANES, (cs + 1) * LANES)
            acc = b_dw_ref[:, lanes] + w_dw_ref[0:1, lanes] * u_buf[cs, r0 + CONV_HIST - (CONV_K - 1):
                                                                    r0 + CONV_HIST - (CONV_K - 1) + CONV_ROWS, :]
            for k in range(1, CONV_K):
                off = r0 + CONV_HIST - (CONV_K - 1) + k
                acc = acc + w_dw_ref[k:k + 1, lanes] * u_buf[cs, off:off + CONV_ROWS, :]
            accs.append(acc)
        y = jnp.concatenate(accs, axis=-1)
        cen = y - jnp.mean(y, axis=-1, keepdims=True)
        var = jnp.mean(cen * cen, axis=-1, keepdims=True)
        y = cen * lax.rsqrt(var + EPS) * g_ln_ref[...] + b_ln_ref[...]
        c_buf[r0:r0 + CONV_ROWS, :] = (y * jax.nn.sigmoid(y)).astype(_BF16)

    for p in range(t // PAIR):
        attention_pair(p)
        for rb in range(PAIR // CONV_ROWS):
            conv_rows(p * PAIR + rb * CONV_ROWS)

    a = _dot(a_buf[...], w_ao_ref[...])
    cb = _dot(c_buf[...], w_co_ref[...]) + b_co_ref[...]

    gate_a = jax.nn.sigmoid(proj(3 * D_ATTN + 2 * D_CONV, D_MODEL))
    gate_b = jax.nn.sigmoid(proj(3 * D_ATTN + 2 * D_CONV + D_MODEL, D_MODEL))
    y = _dot((gate_a * a + gate_b * cb).astype(_BF16), w_mo_ref[...])
    o_ref[...] = x + gate * (_rms(y) * gpost_ref[...])


def _rel_bias_table(rel_bias):
    h = rel_bias.shape[0]
    period = PAIR_KEYS + 1
    i = np.arange(period)
    key_minus_query = np.where(i < BAND, i, i - period)
    idx = np.clip(CTX - key_minus_query, -MAX_REL, MAX_REL) + MAX_REL
    vec = rel_bias[:, idx]
    blk = jnp.tile(vec, (1, CHUNK))[:, :CHUNK * PAIR_KEYS].reshape(h, CHUNK, PAIR_KEYS)[:, :, :BAND]
    pad = jnp.full((h, CHUNK, PAIR_KEYS - BAND), NEG_INF, blk.dtype)
    first_chunk = jnp.concatenate([blk, pad], axis=-1)
    second_chunk = jnp.concatenate([pad, blk], axis=-1)
    table = jnp.concatenate([first_chunk, second_chunk], axis=1)
    return table.reshape(N_SLABS, HEADS_PER_SLAB * PAIR, PAIR_KEYS)


def _const_spec(shape):
    zeros = (0,) * len(shape)
    return pl.BlockSpec(shape, lambda b, s: zeros, pipeline_mode=pl.Buffered(1))


def _mixer(x, mod, g_pre, g_post, w_in, b_in, bias_tab, w_ao, w_dw, b_dw, g_ln, b_ln, w_co, b_co, w_mo):
    bsz, seq, d = x.shape
    t = TILE_MIX
    consts = [g_pre, g_post, w_in, b_in, bias_tab, w_ao, w_dw, b_dw, g_ln, b_ln, w_co, b_co, w_mo]
    return pl.pallas_call(
        _mixer_kernel,
        out_shape=jax.ShapeDtypeStruct(x.shape, _F32),
        grid=(bsz, seq // t),
        in_specs=[
            pl.BlockSpec((None, t, d), lambda b, s: (b, s, 0)),
            pl.BlockSpec((None, 3, d), lambda b, s: (b, 0, 0)),
        ] + [_const_spec(a.shape) for a in consts],
        out_specs=pl.BlockSpec((None, t, d), lambda b, s: (b, s, 0)),
        scratch_shapes=[
            pltpu.VMEM((t, d), _BF16),
            pltpu.VMEM((t // PAIR, N_SLABS, HEADS_PER_SLAB * PAIR, SLAB), _BF16),
            pltpu.VMEM((CTX + t, D_ATTN), _BF16),
            pltpu.VMEM((CTX + t, D_ATTN), _BF16),
            pltpu.VMEM((t, D_ATTN), _BF16),
            pltpu.VMEM((D_CONV // LANES, CONV_HIST + t, LANES), _F32),
            pltpu.VMEM((t, D_CONV), _BF16),
        ],
        compiler_params=pltpu.CompilerParams(
            dimension_semantics=("arbitrary", "arbitrary"),
            vmem_limit_bytes=VMEM_LIMIT_BYTES),
        name="mixer",
    )(x, mod, *consts)


def _ffn_kernel(x_ref, mod_ref, gpre_ref, gpost_ref, w_up_ref, w_dw_ref, b_dw_ref, w_down_ref,
                o_ref, h_buf, act_buf, up_buf):
    t = TILE_FFN
    s = pl.program_id(1)

    @pl.when(s == 0)
    def _():
        up_buf[:, 0:FFN_HIST, :] = jnp.zeros((up_buf.shape[0], FFN_HIST, LANES), _F32)

    @pl.when(s > 0)
    def _():
        up_buf[:, 0:FFN_HIST, :] = up_buf[:, t:t + FFN_HIST, :]

    shift = mod_ref[0:1, :]
    scale = mod_ref[1:2, :]
    gate = mod_ref[2:3, :]

    x = x_ref[...]
    h_buf[...] = (_rms(x) * gpre_ref[...] * (1.0 + scale) + shift).astype(_BF16)

    def conv_cols(c0):
        up = _dot(h_buf[...], w_up_ref[:, c0:c0 + FFN_COLS])
        ys = []
        for ws in range(FFN_COLS // LANES):
            cols = slice(c0 + ws * LANES, c0 + (ws + 1) * LANES)
            slab = c0 // LANES + ws
            up_s = up[:, ws * LANES:(ws + 1) * LANES]
            up_buf[slab, FFN_HIST:FFN_HIST + t, :] = up_s
            y = b_dw_ref[:, cols] + w_dw_ref[FFN_CONV_K - 1:FFN_CONV_K, cols] * up_s
            for k in range(FFN_CONV_K - 1):
                off = FFN_HIST - (FFN_CONV_K - 1) + k
                y = y + w_dw_ref[k:k + 1, cols] * up_buf[slab, off:off + t, :]
            ys.append(y)
        return jnp.concatenate(ys, axis=-1)

    for c in range(D_FF // FFN_COLS):
        val = conv_cols(c * FFN_COLS)
        gt = conv_cols(D_FF + c * FFN_COLS)
        act_buf[:, c * FFN_COLS:(c + 1) * FFN_COLS] = (jax.nn.gelu(gt) * val).astype(_BF16)

    y = _dot(act_buf[...], w_down_ref[...])
    o_ref[...] = x + gate * (_rms(y) * gpost_ref[...])


def _ffn(x, mod, g_pre, g_post, w_up, w_dw, b_dw, w_down):
    bsz, seq, d = x.shape
    t = TILE_FFN
    consts = [g_pre, g_post, w_up, w_dw, b_dw, w_down]
    return pl.pallas_call(
        _ffn_kernel,
        out_shape=jax.ShapeDtypeStruct(x.shape, _F32),
        grid=(bsz, seq // t),
        in_specs=[
            pl.BlockSpec((None, t, d), lambda b, s: (b, s, 0)),
            pl.BlockSpec((None, 3, d), lambda b, s: (b, 0, 0)),
        ] + [_const_spec(a.shape) for a in consts],
        out_specs=pl.BlockSpec((None, t, d), lambda b, s: (b, s, 0)),
        scratch_shapes=[
            pltpu.VMEM((t, d), _BF16),
            pltpu.VMEM((t, D_FF), _BF16),
            pltpu.VMEM((2 * D_FF // LANES, FFN_HIST + t, LANES), _F32),
        ],
        compiler_params=pltpu.CompilerParams(
            dimension_semantics=("arbitrary", "arbitrary"),
            vmem_limit_bytes=VMEM_LIMIT_BYTES),
        name="ffn",
    )(x, mod, *consts)


def kernel(x, c, w_ada, b_ada, g_pre_mix, g_post_mix, w_in, b_in, rel_bias, w_attn_o, w_dw_conv,
           b_dw_conv, g_conv_ln, b_conv_ln, w_conv_o, b_conv_o, w_mix_o, g_pre_ffn, g_post_ffn,
           w_up, w_dw_ffn, b_dw_ffn, w_down):
    depth = w_ada.shape[0]
    bsz = x.shape[0]
    assert x.shape[1] % TILE_MIX == 0 and x.shape[1] % TILE_FFN == 0
    row = lambda v: v.reshape(1, -1)
    for l in range(depth):
        mod = _adaln(c, w_ada[l], b_ada[l]).reshape(bsz, 6, D_MODEL)
        x = _mixer(x, mod[:, 0:3], row(g_pre_mix[l]), row(g_post_mix[l]),
                   w_in[l].astype(_BF16), row(b_in[l]), _rel_bias_table(rel_bias[l]),
                   w_attn_o[l].astype(_BF16), w_dw_conv[l], row(b_dw_conv[l]),
                   row(g_conv_ln[l]), row(b_conv_ln[l]), w_conv_o[l].astype(_BF16),
                   row(b_conv_o[l]), w_mix_o[l].astype(_BF16))
        x = _ffn(x, mod[:, 3:6], row(g_pre_ffn[l]), row(g_post_ffn[l]),
                 w_up[l].astype(_BF16), w_dw_ffn[l], row(b_dw_ffn[l]), w_down[l].astype(_BF16))
    return x
```

```python
import jax
import jax.numpy as jnp
import numpy as np
from jax import lax
from jax.experimental import pallas as pl
from jax.experimental.pallas import tpu as pltpu

D_MODEL = 1024
CHUNK = 64
LEFT_CHUNKS = 8
CTX = LEFT_CHUNKS * CHUNK
BAND = CTX + CHUNK
N_HEADS = 8
HEAD_DIM = 64
D_ATTN = N_HEADS * HEAD_DIM
D_CONV = 512
CONV_K = 31
MAX_REL = 128
D_FF = 2816
FFN_CONV_K = 3
EPS = 1e-6
NEG_INF = -1e30

LANES = 128
PAIR = 2 * CHUNK
PAIR_KEYS = PAIR + CTX
SLAB = 256
HEADS_PER_SLAB = SLAB // HEAD_DIM
N_SLABS = D_ATTN // SLAB
CONV_HIST = 32
CONV_ROWS = 64
FFN_HIST = 8
FFN_COLS = 256
GATE_COLS = 512

TILE_MIX = 512
TILE_FFN = 512
VMEM_LIMIT_BYTES = 56 * 1024 * 1024

_F32 = jnp.float32
_BF16 = jnp.bfloat16


def _dot(a, b):
    return jnp.dot(a, b, preferred_element_type=_F32)


def _dot_nt(a, b):
    return lax.dot_general(a, b, (((1,), (1,)), ((), ())), preferred_element_type=_F32)


def _rms(x):
    return x * lax.rsqrt(jnp.mean(x * x, axis=-1, keepdims=True) + EPS)


def _adaln_kernel(c_ref, w_ref, b_ref, o_ref):
    c = c_ref[...]
    o_ref[...] = _dot(c * jax.nn.sigmoid(c), w_ref[...]) + b_ref[...]


def _adaln(c, w_ada, b_ada):
    b, d = c.shape
    n = w_ada.shape[1]
    tn = 1024
    return pl.pallas_call(
        _adaln_kernel,
        out_shape=jax.ShapeDtypeStruct((b, n), _F32),
        grid=(n // tn,),
        in_specs=[
            pl.BlockSpec((b, d), lambda j: (0, 0)),
            pl.BlockSpec((d, tn), lambda j: (0, j)),
            pl.BlockSpec((1, tn), lambda j: (0, j)),
        ],
        out_specs=pl.BlockSpec((b, tn), lambda j: (0, j)),
        name="adaln",
    )(c, w_ada, b_ada.reshape(1, n))


def _mixer_kernel(x_ref, mod_ref, gpre_ref, gpost_ref, w_in_ref, b_in_ref, bias_ref,
                  w_ao_ref, w_dw_ref, b_dw_ref, g_ln_ref, b_ln_ref, w_co_ref, b_co_ref,
                  w_mo_ref, o_ref,
                  h_buf, q_buf, k_buf, v_buf, a_buf, u_buf, c_buf, g_buf):
    t = TILE_MIX
    s = pl.program_id(1)
    n_conv_slabs = D_CONV // LANES

    @pl.when(s == 0)
    def _():
        k_buf[0:CTX, :] = jnp.zeros((CTX, D_ATTN), _BF16)
        v_buf[0:CTX, :] = jnp.zeros((CTX, D_ATTN), _BF16)
        u_buf[:, 0:CONV_HIST, :] = jnp.zeros((n_conv_slabs, CONV_HIST, LANES), _F32)

    @pl.when(s > 0)
    def _():
        k_buf[0:CTX, :] = k_buf[t:t + CTX, :]
        v_buf[0:CTX, :] = v_buf[t:t + CTX, :]
        u_buf[:, 0:CONV_HIST, :] = u_buf[:, t:t + CONV_HIST, :]

    shift = mod_ref[0:1, :]
    scale = mod_ref[1:2, :]
    gate = mod_ref[2:3, :]

    half = t // 2
    glu0 = 3 * D_ATTN
    gate0 = glu0 + 2 * D_CONV

    def proj(c0, width, rows=slice(None)):
        return _dot(h_buf[rows, :], w_in_ref[:, c0:c0 + width]) + b_in_ref[:, c0:c0 + width]

    for r0 in range(0, t, half):
        rows = slice(r0, r0 + half)
        h_buf[rows, :] = (_rms(x_ref[rows, :]) * gpre_ref[...] * (1.0 + scale) + shift).astype(_BF16)
        u = proj(glu0, D_CONV, rows) * jax.nn.sigmoid(proj(glu0 + D_CONV, D_CONV, rows))
        for cs in range(n_conv_slabs):
            u_buf[cs, CONV_HIST + r0:CONV_HIST + r0 + half, :] = u[:, cs * LANES:(cs + 1) * LANES]

    lane = lax.broadcasted_iota(jnp.int32, (PAIR, SLAB), 1)
    head_lanes = [(lane >= hh * HEAD_DIM) & (lane < (hh + 1) * HEAD_DIM) for hh in range(HEADS_PER_SLAB)]

    def project_q(sl):
        q = proj(sl * SLAB, SLAB) * (HEAD_DIM ** -0.5)
        for p in range(t // PAIR):
            qp = q[p * PAIR:(p + 1) * PAIR, :]
            for hh in range(HEADS_PER_SLAB):
                q_buf[p, sl, hh * PAIR:(hh + 1) * PAIR, :] = jnp.where(head_lanes[hh], qp, 0.0).astype(_BF16)

    def project_k():
        k_buf[CTX:CTX + t, :] = proj(D_ATTN, D_ATTN).astype(_BF16)

    def project_v():
        v_buf[CTX:CTX + t, :] = proj(2 * D_ATTN, D_ATTN).astype(_BF16)

    def project_gate(g, c):
        cols = slice(c * GATE_COLS, (c + 1) * GATE_COLS)
        g_buf[g, :, cols] = jax.nn.sigmoid(proj(gate0 + g * D_MODEL + c * GATE_COLS, GATE_COLS))

    key_col = lax.broadcasted_iota(jnp.int32, (1, PAIR_KEYS), 1)

    def attention_pair(p):
        r0 = p * PAIR
        seq_mask = jnp.where(s * t - CTX + r0 + key_col >= 0, 0.0, NEG_INF)
        for sl in range(N_SLABS):
            c0 = sl * SLAB
            ks = k_buf[r0:r0 + PAIR_KEYS, c0:c0 + SLAB]
            vs = v_buf[r0:r0 + PAIR_KEYS, c0:c0 + SLAB]
            sc = _dot_nt(q_buf[p, sl], ks) + bias_ref[sl] + seq_mask
            e = jnp.exp(sc - jnp.max(sc, axis=-1, keepdims=True))
            l = jnp.sum(e, axis=-1, keepdims=True)
            o = _dot(e.astype(_BF16), vs) * (1.0 / l)
            out = jnp.where(head_lanes[0], o[0:PAIR, :], 0.0)
            for hh in range(1, HEADS_PER_SLAB):
                out = out + jnp.where(head_lanes[hh], o[hh * PAIR:(hh + 1) * PAIR, :], 0.0)
            a_buf[r0:r0 + PAIR, c0:c0 + SLAB] = out.astype(_BF16)

    def conv_rows(r0):
        accs = []
        for cs in range(n_conv_slabs):
            lanes = slice(cs * LANES, (cs + 1) * LANES)
            acc = b_dw_ref[:, lanes] + w_dw_ref[0:1, lanes] * u_buf[cs, r0 + CONV_HIST - (CONV_K - 1):
                                                                    r0 + CONV_HIST - (CONV_K - 1) + CONV_ROWS, :]
            for k in range(1, CONV_K):
                off = r0 + CONV_HIST - (CONV_K - 1) + k
                acc = acc + w_dw_ref[k:k + 1, lanes] * u_buf[cs, off:off + CONV_ROWS, :]
            accs.append(acc)
        y = jnp.concatenate(accs, axis=-1)
        cen = y - jnp.mean(y, axis=-1, keepdims=True)
        var = jnp.mean(cen * cen, axis=-1, keepdims=True)
        y = cen * lax.rsqrt(var + EPS) * g_ln_ref[...] + b_ln_ref[...]
        c_buf[r0:r0 + CONV_ROWS, :] = (y * jax.nn.sigmoid(y)).astype(_BF16)

    proj_steps = [lambda: project_q(0), lambda: project_q(1), project_k, project_v]
    proj_steps += [lambda g=g, c=c: project_gate(g, c) for g in range(2) for c in range(D_MODEL // GATE_COLS)]
    assert len(proj_steps) == t // CONV_ROWS
    for i, step in enumerate(proj_steps):
        step()
        conv_rows(i * CONV_ROWS)

    for p in range(t // PAIR):
        attention_pair(p)

    a = _dot(a_buf[...], w_ao_ref[...])
    cb = _dot(c_buf[...], w_co_ref[...]) + b_co_ref[...]

    for r0 in range(0, t, half):
        rows = slice(r0, r0 + half)
        mix = g_buf[0, rows, :] * a[rows, :] + g_buf[1, rows, :] * cb[rows, :]
        y = _dot(mix.astype(_BF16), w_mo_ref[...])
        o_ref[rows, :] = x_ref[rows, :] + gate * (_rms(y) * gpost_ref[...])


def _rel_bias_table(rel_bias):
    h = rel_bias.shape[0]
    period = PAIR_KEYS + 1
    i = np.arange(period)
    key_minus_query = np.where(i < BAND, i, i - period)
    idx = np.clip(CTX - key_minus_query, -MAX_REL, MAX_REL) + MAX_REL
    vec = rel_bias[:, idx]
    blk = jnp.tile(vec, (1, CHUNK))[:, :CHUNK * PAIR_KEYS].reshape(h, CHUNK, PAIR_KEYS)[:, :, :BAND]
    pad = jnp.full((h, CHUNK, PAIR_KEYS - BAND), NEG_INF, blk.dtype)
    first_chunk = jnp.concatenate([blk, pad], axis=-1)
    second_chunk = jnp.concatenate([pad, blk], axis=-1)
    table = jnp.concatenate([first_chunk, second_chunk], axis=1)
    return table.reshape(N_SLABS, HEADS_PER_SLAB * PAIR, PAIR_KEYS)


def _const_spec(shape):
    zeros = (0,) * len(shape)
    return pl.BlockSpec(shape, lambda b, s: zeros, pipeline_mode=pl.Buffered(1))


def _mixer(x, mod, g_pre, g_post, w_in, b_in, bias_tab, w_ao, w_dw, b_dw, g_ln, b_ln, w_co, b_co, w_mo):
    bsz, seq, d = x.shape
    t = TILE_MIX
    consts = [g_pre, g_post, w_in, b_in, bias_tab, w_ao, w_dw, b_dw, g_ln, b_ln, w_co, b_co, w_mo]
    return pl.pallas_call(
        _mixer_kernel,
        out_shape=jax.ShapeDtypeStruct(x.shape, _F32),
        grid=(bsz, seq // t),
        in_specs=[
            pl.BlockSpec((None, t, d), lambda b, s: (b, s, 0)),
            pl.BlockSpec((None, 3, d), lambda b, s: (b, 0, 0)),
        ] + [_const_spec(a.shape) for a in consts],
        out_specs=pl.BlockSpec((None, t, d), lambda b, s: (b, s, 0)),
        scratch_shapes=[
            pltpu.VMEM((t, d), _BF16),
            pltpu.VMEM((t // PAIR, N_SLABS, HEADS_PER_SLAB * PAIR, SLAB), _BF16),
            pltpu.VMEM((CTX + t, D_ATTN), _BF16),
            pltpu.VMEM((CTX + t, D_ATTN), _BF16),
            pltpu.VMEM((t, D_ATTN), _BF16),
            pltpu.VMEM((D_CONV // LANES, CONV_HIST + t, LANES), _F32),
            pltpu.VMEM((t, D_CONV), _BF16),
            pltpu.VMEM((2, t, d), _F32),
        ],
        compiler_params=pltpu.CompilerParams(
            dimension_semantics=("arbitrary", "arbitrary"),
            vmem_limit_bytes=VMEM_LIMIT_BYTES),
        name="mixer",
    )(x, mod, *consts)


def _ffn_kernel(x_ref, mod_ref, gpre_ref, gpost_ref, w_up_ref, w_dw_ref, b_dw_ref, w_down_ref,
                o_ref, h_buf, act_buf, up_buf):
    t = TILE_FFN
    s = pl.program_id(1)

    @pl.when(s == 0)
    def _():
        up_buf[:, 0:FFN_HIST, :] = jnp.zeros((up_buf.shape[0], FFN_HIST, LANES), _F32)

    @pl.when(s > 0)
    def _():
        up_buf[:, 0:FFN_HIST, :] = up_buf[:, t:t + FFN_HIST, :]

    shift = mod_ref[0:1, :]
    scale = mod_ref[1:2, :]
    gate = mod_ref[2:3, :]

    half = t // 2

    def conv_cols(c0, r0, n):
        up = _dot(h_buf[r0:r0 + n, :], w_up_ref[:, c0:c0 + FFN_COLS])
        ys = []
        for ws in range(FFN_COLS // LANES):
            cols = slice(c0 + ws * LANES, c0 + (ws + 1) * LANES)
            slab = c0 // LANES + ws
            up_s = up[:, ws * LANES:(ws + 1) * LANES]
            up_buf[slab, FFN_HIST + r0:FFN_HIST + r0 + n, :] = up_s
            y = b_dw_ref[:, cols] + w_dw_ref[FFN_CONV_K - 1:FFN_CONV_K, cols] * up_s
            for k in range(FFN_CONV_K - 1):
                off = FFN_HIST + r0 - (FFN_CONV_K - 1) + k
                y = y + w_dw_ref[k:k + 1, cols] * up_buf[slab, off:off + n, :]
            ys.append(y)
        return jnp.concatenate(ys, axis=-1)

    def act_cols(c, r0, n):
        val = conv_cols(c * FFN_COLS, r0, n)
        gt = conv_cols(D_FF + c * FFN_COLS, r0, n)
        act_buf[r0:r0 + n, c * FFN_COLS:(c + 1) * FFN_COLS] = (jax.nn.gelu(gt) * val).astype(_BF16)

    for r0 in range(0, t, half):
        rows = slice(r0, r0 + half)
        h_buf[rows, :] = (_rms(x_ref[rows, :]) * gpre_ref[...] * (1.0 + scale) + shift).astype(_BF16)
        act_cols(0, r0, half)
    for c in range(1, D_FF // FFN_COLS):
        act_cols(c, 0, t)

    for r0 in range(0, t, half):
        rows = slice(r0, r0 + half)
        y = _dot(act_buf[rows, :], w_down_ref[...])
        o_ref[rows, :] = x_ref[rows, :] + gate * (_rms(y) * gpost_ref[...])


def _ffn(x, mod, g_pre, g_post, w_up, w_dw, b_dw, w_down):
    bsz, seq, d = x.shape
    t = TILE_FFN
    consts = [g_pre, g_post, w_up, w_dw, b_dw, w_down]
    return pl.pallas_call(
        _ffn_kernel,
        out_shape=jax.ShapeDtypeStruct(x.shape, _F32),
        grid=(bsz, seq // t),
        in_specs=[
            pl.BlockSpec((None, t, d), lambda b, s: (b, s, 0)),
            pl.BlockSpec((None, 3, d), lambda b, s: (b, 0, 0)),
        ] + [_const_spec(a.shape) for a in consts],
        out_specs=pl.BlockSpec((None, t, d), lambda b, s: (b, s, 0)),
        scratch_shapes=[
            pltpu.VMEM((t, d), _BF16),
            pltpu.VMEM((t, D_FF), _BF16),
            pltpu.VMEM((2 * D_FF // LANES, FFN_HIST + t, LANES), _F32),
        ],
        compiler_params=pltpu.CompilerParams(
            dimension_semantics=("arbitrary", "arbitrary"),
            vmem_limit_bytes=VMEM_LIMIT_BYTES),
        name="ffn",
    )(x, mod, *consts)


def kernel(x, c, w_ada, b_ada, g_pre_mix, g_post_mix, w_in, b_in, rel_bias, w_attn_o, w_dw_conv,
           b_dw_conv, g_conv_ln, b_conv_ln, w_conv_o, b_conv_o, w_mix_o, g_pre_ffn, g_post_ffn,
           w_up, w_dw_ffn, b_dw_ffn, w_down):
    depth = w_ada.shape[0]
    bsz = x.shape[0]
    assert x.shape[1] % TILE_MIX == 0 and x.shape[1] % TILE_FFN == 0
    row = lambda v: v.reshape(1, -1)
    for l in range(depth):
        mod = _adaln(c, w_ada[l], b_ada[l]).reshape(bsz, 6, D_MODEL)
        x = _mixer(x, mod[:, 0:3], row(g_pre_mix[l]), row(g_post_mix[l]),
                   w_in[l].astype(_BF16), row(b_in[l]), _rel_bias_table(rel_bias[l]),
                   w_attn_o[l].astype(_BF16), w_dw_conv[l], row(b_dw_conv[l]),
                   row(g_conv_ln[l]), row(b_conv_ln[l]), w_conv_o[l].astype(_BF16),
                   row(b_conv_o[l]), w_mix_o[l].astype(_BF16))
        x = _ffn(x, mod[:, 3:6], row(g_pre_ffn[l]), row(g_post_ffn[l]),
                 w_up[l].astype(_BF16), w_dw_ffn[l], row(b_dw_ffn[l]), w_down[l].astype(_BF16))
    return x
```
